```python
import math
import jax, jax.numpy as jnp
from jax import lax
import numpy as np

D_MODEL = 2048
BATCH = 2
SEQ = 4096
DEPTH = 2
DEC_BATCH = 32
DEC_SEQ = 4
PAST_LEN = 8192
PAGE_SIZE = 128

N_MIXERS = 4
GROUP_W = D_MODEL // N_MIXERS
SSM_CH_PER_GROUP = 16
SSM_GROUPS = GROUP_W // SSM_CH_PER_GROUP
SSM_STATE = 64
CONV_K = 3
POOL_WINDOWS = (2, 4, 8, 16)
POOL_CH = GROUP_W // len(POOL_WINDOWS)
POOL_BUF = max(POOL_WINDOWS) - 1
HEAD_DIM = 64
NSA_HEADS = GROUP_W // HEAD_DIM
KV_HEADS = 2
Q_PER_KV = NSA_HEADS // KV_HEADS
BLOCK = 64
TOP_N = 16
WINDOW = 512
Q_BLOCK = 128
N_BRANCH_KV = 6
KV_COLS = N_BRANCH_KV * KV_HEADS * HEAD_DIM
IN_COLS = 6 * GROUP_W + KV_COLS + 3 * NSA_HEADS
D_FF = 4 * D_MODEL
ALPHA = (2.0 * DEPTH) ** 0.25
BETA = (8.0 * DEPTH) ** -0.25
NEG_INF = -1e30
FORCED_SCORE = Q_PER_KV + 1.0
LN_EPS = 1e-5

kernel_name = 'hybrid_s5_conv_pool_nsa_step'


def layer_norm(x, g, b):
    xf = x.astype(jnp.float32)
    mu = jnp.mean(xf, axis=-1, keepdims=True)
    var = jnp.mean(jnp.square(xf - mu), axis=-1, keepdims=True)
    y = (xf - mu) * lax.rsqrt(var + LN_EPS) * g.astype(jnp.float32) + b.astype(jnp.float32)
    return y.astype(x.dtype)


def _ssm_combine(e1, e2):
    a1r, a1i, b1r, b1i = e1
    a2r, a2i, b2r, b2i = e2
    ar = a2r * a1r - a2i * a1i
    ai = a2r * a1i + a2i * a1r
    br = a2r * b1r - a2i * b1i + b2r
    bi = a2r * b1i + a2i * b1r + b2i
    return (ar, ai, br, bi)


def s5_mixer(u, h0_re, h0_im, a_re, a_im, b_re, b_im, c_re, c_im, d, log_dt, w_glu, b_glu):
    f32 = jnp.float32
    B, T, _ = u.shape
    a_re = a_re.astype(f32)
    a_im = a_im.astype(f32)
    dt = jnp.exp(log_dt.astype(f32))[:, None]
    mag = jnp.exp(dt * a_re)
    ab_re = mag * jnp.cos(dt * a_im)
    ab_im = mag * jnp.sin(dt * a_im)
    den = a_re * a_re + a_im * a_im
    nr = ab_re - 1.0
    f_re = (nr * a_re + ab_im * a_im) / den
    f_im = (ab_im * a_re - nr * a_im) / den
    b_re = b_re.astype(f32)
    b_im = b_im.astype(f32)
    bb_re = f_re[..., None] * b_re - f_im[..., None] * b_im
    bb_im = f_re[..., None] * b_im + f_im[..., None] * b_re
    uf = u.astype(f32)
    ug = uf.reshape(B, T, SSM_GROUPS, SSM_CH_PER_GROUP)
    bu_re = jnp.einsum('gnc,btgc->btgn', bb_re, ug)
    bu_im = jnp.einsum('gnc,btgc->btgn', bb_im, ug)
    shp = bu_re.shape
    elems = (jnp.broadcast_to(ab_re, shp), jnp.broadcast_to(ab_im, shp), bu_re, bu_im)
    acc_re, acc_im, s_re, s_im = lax.associative_scan(_ssm_combine, elems, axis=1)
    h0r = h0_re.astype(f32)[:, None]
    h0i = h0_im.astype(f32)[:, None]
    h_re = s_re + acc_re * h0r - acc_im * h0i
    h_im = s_im + acc_re * h0i + acc_im * h0r
    y = (jnp.einsum('gcn,btgn->btgc', c_re.astype(f32), h_re)
         - jnp.einsum('gcn,btgn->btgc', c_im.astype(f32), h_im))
    y = y.reshape(B, T, GROUP_W) + d.astype(f32) * uf
    y = jax.nn.gelu(y)
    y = y * jax.nn.sigmoid(y @ w_glu.astype(f32) + b_glu.astype(f32))
    return y.astype(u.dtype), h_re[:, -1], h_im[:, -1]


def short_conv_mixer(g_b, g_c, v, prev, conv_w, conv_b):
    T = v.shape[1]
    cv = g_c * v
    ext = jnp.concatenate([prev.astype(cv.dtype), cv], axis=1)
    conv = conv_b
    for j in range(CONV_K):
        conv = conv + conv_w[j] * ext[:, j:j + T]
    return g_b * conv, ext[:, -(CONV_K - 1):]


def pool_mixer(u, prev, n_prev_valid, pool_w, pool_scale):
    f32 = jnp.float32
    B, T, C = u.shape
    ext = jnp.concatenate([prev.astype(u.dtype), u], axis=1)
    cs = jnp.concatenate([jnp.zeros((B, 1, C), f32), jnp.cumsum(ext.astype(f32), axis=1)], axis=1)
    hi = cs[:, POOL_BUF + 1:]
    t = jnp.arange(T)
    outs = []
    for gi, w in enumerate(POOL_WINDOWS):
        sl = slice(gi * POOL_CH, (gi + 1) * POOL_CH)
        lo = cs[:, POOL_BUF + 1 - w:POOL_BUF + 1 - w + T, sl]
        cnt = jnp.minimum(t + 1 + n_prev_valid, w).astype(f32)[None, :, None]
        outs.append((hi[..., sl] - lo) / cnt)
    pooled = jnp.concatenate(outs, axis=-1) - u.astype(f32)
    y = jnp.einsum('btgc,gcd->btgd', pooled.reshape(B, T, len(POOL_WINDOWS), POOL_CH), pool_w.astype(f32))
    y = y.reshape(B, T, C) * pool_scale.astype(f32)
    return y.astype(u.dtype), ext[:, -POOL_BUF:]


def compress_blocks(k, w):
    B, L, G, D = k.shape
    kb = k.reshape(B, L // BLOCK, BLOCK, G, D)
    return jnp.einsum('bjkgd,kd->bjgd', kb, w)


def nsa_compressed_branch(q, q_pos, k_cmp, v_cmp, w_ck, w_cv):
    B, T, H, D = q.shape
    kc = compress_blocks(k_cmp, w_ck)
    vc = compress_blocks(v_cmp, w_cv)
    NB = kc.shape[1]
    qg = q.reshape(B, T, KV_HEADS, Q_PER_KV, D)
    s = jnp.einsum('btgrd,bjgd->bgrtj', qg, kc).astype(jnp.float32) * (HEAD_DIM ** -0.5)
    j = jnp.arange(NB)
    complete = ((j + 1) * BLOCK - 1)[None, :] <= q_pos[:, None]
    s = jnp.where(complete, s, NEG_INF)
    p = jnp.where(complete, jax.nn.softmax(s, axis=-1), 0.0)
    o = jnp.einsum('bgrtj,bjgd->btgrd', p.astype(vc.dtype), vc).reshape(B, T, H, D)
    imp = jnp.sum(p, axis=2)
    cur = q_pos // BLOCK
    forced = (j[None, :] == 0) | (j[None, :] == cur[:, None]) | (j[None, :] == cur[:, None] - 1)
    started = (j * BLOCK)[None, :] <= q_pos[:, None]
    score = jnp.where(forced, FORCED_SCORE, jnp.where(started, imp, -1.0))
    _, idx = lax.top_k(score, min(TOP_N, NB))
    return o, idx


def nsa_selected_branch(q, q_pos, k_sel, v_sel, idx):
    B, T, H, D = q.shape
    L = k_sel.shape[1]
    kb = k_sel.reshape(B, L // BLOCK, BLOCK, KV_HEADS, D).transpose(0, 3, 1, 2, 4)
    vb = v_sel.reshape(B, L // BLOCK, BLOCK, KV_HEADS, D).transpose(0, 3, 1, 2, 4)
    qb = min(Q_BLOCK, T)
    nq = T // qb
    n_sel = idx.shape[-1]
    gather = jax.vmap(jax.vmap(lambda tbl, ii: tbl[ii]))

    def one(args):
        qc, pc, ic = args
        kg = gather(kb, ic)
        vg = gather(vb, ic)
        kpos = ic[..., None] * BLOCK + jnp.arange(BLOCK)
        s = jnp.einsum('bqgrd,bgqnkd->bgrqnk', qc.reshape(B, qb, KV_HEADS, Q_PER_KV, D), kg)
        s = s.astype(jnp.float32) * (HEAD_DIM ** -0.5)
        mask = (kpos <= pc[None, None, :, None, None])[:, :, None]
        s = jnp.where(mask, s, NEG_INF).reshape(B, KV_HEADS, Q_PER_KV, qb, n_sel * BLOCK)
        p = jax.nn.softmax(s, axis=-1).reshape(B, KV_HEADS, Q_PER_KV, qb, n_sel, BLOCK)
        o = jnp.einsum('bgrqnk,bgqnkd->bqgrd', p.astype(vg.dtype), vg)
        return o.reshape(B, qb, H, D)

    qs = q.reshape(B, nq, qb, H, D).transpose(1, 0, 2, 3, 4)
    ps = q_pos.reshape(nq, qb)
    ids = idx.reshape(B, KV_HEADS, nq, qb, n_sel).transpose(2, 0, 1, 3, 4)
    out = lax.map(one, (qs, ps, ids))
    return out.transpose(1, 0, 2, 3, 4).reshape(B, T, H, D)


def nsa_window_branch(q, q_pos, k_ext, v_ext, kpos_ext, n_prev):
    B, T, H, D = q.shape
    qb = min(Q_BLOCK, T)
    nq = T // qb
    span = n_prev + qb

    def one(args):
        c, qc, pc = args
        start = c * qb
        kc = lax.dynamic_slice_in_dim(k_ext, start, span, axis=1)
        vc = lax.dynamic_slice_in_dim(v_ext, start, span, axis=1)
        kp = lax.dynamic_slice_in_dim(kpos_ext, start, span, axis=0)
        s = jnp.einsum('bqgrd,bkgd->bgrqk', qc.reshape(B, qb, KV_HEADS, Q_PER_KV, D), kc)
        s = s.astype(jnp.float32) * (HEAD_DIM ** -0.5)
        rel = pc[:, None] - kp[None, :]
        mask = (rel >= 0) & (rel < WINDOW) & (kp[None, :] >= 0)
        p = jax.nn.softmax(jnp.where(mask, s, NEG_INF), axis=-1)
        o = jnp.einsum('bgrqk,bkgd->bqgrd', p.astype(vc.dtype), vc)
        return o.reshape(B, qb, H, D)

    qs = q.reshape(B, nq, qb, H, D).transpose(1, 0, 2, 3, 4)
    ps = q_pos.reshape(nq, qb)
    out = lax.map(one, (jnp.arange(nq), qs, ps))
    return out.transpose(1, 0, 2, 3, 4).reshape(B, T, H, D)


def trunk_layer(x, lp, pos0, h0_re, h0_im, conv_prev, pool_prev, pool_prev_valid, past_kv, win_prev):
    B, T, _ = x.shape
    q_pos = pos0 + jnp.arange(T)
    h = x @ lp['w_in']
    sizes = (GROUP_W, GROUP_W, GROUP_W, GROUP_W, GROUP_W, GROUP_W, KV_COLS)
    cuts = [int(c) for c in np.cumsum(sizes)]
    u_ssm, g_b, g_c, v_conv, u_pool, q, kv, g_nsa = jnp.split(h, cuts, axis=-1)
    y_a, h_re, h_im = s5_mixer(u_ssm, h0_re, h0_im, lp['ssm_a_re'], lp['ssm_a_im'], lp['ssm_b_re'],
                               lp['ssm_b_im'], lp['ssm_c_re'], lp['ssm_c_im'], lp['ssm_d'],
                               lp['ssm_log_dt'], lp['ssm_w_glu'], lp['ssm_b_glu'])
    y_b, conv_new = short_conv_mixer(g_b, g_c, v_conv, conv_prev, lp['conv_w'], lp['conv_b'])
    y_c, pool_new = pool_mixer(u_pool, pool_prev, pool_prev_valid, lp['pool_w'], lp['pool_scale'])
    q = q.reshape(B, T, NSA_HEADS, HEAD_DIM)
    kv = kv.reshape(B, T, N_BRANCH_KV, KV_HEADS, HEAD_DIM)
    gates = jax.nn.sigmoid(g_nsa.astype(jnp.float32)).reshape(B, T, NSA_HEADS, 3).astype(x.dtype)
    rows = kv[:, :, :4]
    full = jnp.concatenate([past_kv.astype(rows.dtype), rows], axis=1)
    pad = (-full.shape[1]) % BLOCK
    full = jnp.pad(full, ((0, 0), (0, pad), (0, 0), (0, 0), (0, 0)))
    win_ext = jnp.concatenate([win_prev.astype(kv.dtype), kv[:, :, 4:]], axis=1)
    n_prev = win_prev.shape[1]
    kpos_ext = jnp.arange(n_prev + T) + (pos0 - n_prev)
    o_cmp, idx = nsa_compressed_branch(q, q_pos, full[:, :, 0], full[:, :, 1], lp['nsa_w_cmp_k'], lp['nsa_w_cmp_v'])
    o_sel = nsa_selected_branch(q, q_pos, full[:, :, 2], full[:, :, 3], idx)
    o_win = nsa_window_branch(q, q_pos, win_ext[:, :, 0], win_ext[:, :, 1], kpos_ext, n_prev)
    y_d = (gates[..., 0:1] * o_cmp + gates[..., 1:2] * o_sel + gates[..., 2:3] * o_win).reshape(B, T, GROUP_W)
    mix = jnp.concatenate([y_a, y_b, y_c, y_d], axis=-1) @ lp['w_out']
    x = layer_norm(ALPHA * x + mix, lp['ln1_g'], lp['ln1_b'])
    ff = jnp.square(jax.nn.relu(x @ lp['w_up'])) @ lp['w_down']
    x = layer_norm(ALPHA * x + ff, lp['ln2_g'], lp['ln2_b'])
    n_keep = min(n_prev, pos0 + T)
    return x, (rows, win_ext[:, -n_keep:], h_re, h_im, conv_new, pool_new)


def setup_inputs(seed: int = 0) -> dict:
    key = jax.random.key(seed)
    k = jax.random.split(key, 32)
    f32 = jnp.float32

    def nrm(i, shape, scale):
        return scale * jax.random.normal(k[i], shape, f32)

    n_pages = PAST_LEN // PAGE_SIZE
    n_used = DEC_BATCH * n_pages
    n_pool = n_used + max(1, n_used // 4)
    w_buf = min(WINDOW, PAST_LEN)
    page_table = jax.random.permutation(k[8], n_pool)[:n_used].reshape(DEC_BATCH, n_pages).astype(jnp.int32)
    a_im = jnp.broadcast_to(math.pi * jnp.arange(SSM_STATE, dtype=f32), (DEPTH, SSM_GROUPS, SSM_STATE))
    return {
        'x_prompt': nrm(0, (BATCH, SEQ, D_MODEL), 1.0),
        'x_sample': nrm(1, (DEC_BATCH, DEC_SEQ, D_MODEL), 1.0),
        'cache_nsa_kv': nrm(2, (DEPTH, n_pool, PAGE_SIZE, 4, KV_HEADS, HEAD_DIM), 1.0),
        'cache_win_kv': nrm(3, (DEPTH, DEC_BATCH, w_buf, 2, KV_HEADS, HEAD_DIM), 1.0),
        'state_ssm_re': nrm(4, (DEPTH, DEC_BATCH, SSM_GROUPS, SSM_STATE), 0.3),
        'state_ssm_im': nrm(5, (DEPTH, DEC_BATCH, SSM_GROUPS, SSM_STATE), 0.3),
        'state_conv': nrm(6, (DEPTH, DEC_BATCH, CONV_K - 1, GROUP_W), 1.0),
        'state_pool': nrm(7, (DEPTH, DEC_BATCH, POOL_BUF, GROUP_W), 1.0),
        'page_table': page_table,
        'w_in': nrm(9, (DEPTH, D_MODEL, IN_COLS), D_MODEL ** -0.5),
        'ssm_a_re': -0.5 * jnp.exp(nrm(10, (DEPTH, SSM_GROUPS, SSM_STATE), 0.02)),
        'ssm_a_im': a_im,
        'ssm_b_re': nrm(11, (DEPTH, SSM_GROUPS, SSM_STATE, SSM_CH_PER_GROUP), (2 * SSM_CH_PER_GROUP) ** -0.5),
        'ssm_b_im': nrm(12, (DEPTH, SSM_GROUPS, SSM_STATE, SSM_CH_PER_GROUP), (2 * SSM_CH_PER_GROUP) ** -0.5),
        'ssm_c_re': nrm(13, (DEPTH, SSM_GROUPS, SSM_CH_PER_GROUP, SSM_STATE), (2 * SSM_STATE) ** -0.5),
        'ssm_c_im': nrm(14, (DEPTH, SSM_GROUPS, SSM_CH_PER_GROUP, SSM_STATE), (2 * SSM_STATE) ** -0.5),
        'ssm_d': nrm(15, (DEPTH, GROUP_W), 1.0),
        'ssm_log_dt': jax.random.uniform(k[16], (DEPTH, SSM_GROUPS), f32, math.log(1e-3), math.log(1e-1)),
        'ssm_w_glu': nrm(17, (DEPTH, GROUP_W, GROUP_W), GROUP_W ** -0.5),
        'ssm_b_glu': nrm(18, (DEPTH, GROUP_W), 0.01),
        'conv_w': nrm(19, (DEPTH, CONV_K, GROUP_W), CONV_K ** -0.5),
        'conv_b': nrm(20, (DEPTH, GROUP_W), 0.01),
        'pool_w': nrm(21, (DEPTH, len(POOL_WINDOWS), POOL_CH, POOL_CH), POOL_CH ** -0.5),
        'pool_scale': 1.0 + nrm(22, (DEPTH, GROUP_W), 0.1),
        'nsa_w_cmp_k': (1.0 + nrm(23, (DEPTH, BLOCK, HEAD_DIM), 0.1)) / BLOCK,
        'nsa_w_cmp_v': (1.0 + nrm(24, (DEPTH, BLOCK, HEAD_DIM), 0.1)) / BLOCK,
        'w_out': nrm(25, (DEPTH, D_MODEL, D_MODEL), BETA * D_MODEL ** -0.5),
        'ln1_g': 1.0 + nrm(26, (DEPTH, D_MODEL), 0.05),
        'ln1_b': nrm(27, (DEPTH, D_MODEL), 0.01),
        'w_up': nrm(28, (DEPTH, D_MODEL, D_FF), D_MODEL ** -0.5),
        'w_down': nrm(29, (DEPTH, D_FF, D_MODEL), BETA * D_FF ** -0.5),
        'ln2_g': 1.0 + nrm(30, (DEPTH, D_MODEL), 0.05),
        'ln2_b': nrm(31, (DEPTH, D_MODEL), 0.01),
    }


def reference(x_prompt, x_sample, cache_nsa_kv, cache_win_kv, state_ssm_re, state_ssm_im, state_conv,
              state_pool, page_table, w_in, ssm_a_re, ssm_a_im, ssm_b_re, ssm_b_im, ssm_c_re, ssm_c_im,
              ssm_d, ssm_log_dt, ssm_w_glu, ssm_b_glu, conv_w, conv_b, pool_w, pool_scale, nsa_w_cmp_k,
              nsa_w_cmp_v, w_out, ln1_g, ln1_b, w_up, w_down, ln2_g, ln2_b):
    B, T, _ = x_prompt.shape
    Bd = x_sample.shape[0]
    n_pages = page_table.shape[1]
    past_len = n_pages * cache_nsa_kv.shape[2]
    xp = x_prompt
    xs = x_sample
    kv_p, kv_s, win_p, win_s = [], [], [], []
    sre_p, sim_p, sre_s, sim_s = [], [], [], []
    conv_p, conv_s, pool_p, pool_s = [], [], [], []
    for l in range(DEPTH):
        lp = {'w_in': w_in[l], 'ssm_a_re': ssm_a_re[l], 'ssm_a_im': ssm_a_im[l], 'ssm_b_re': ssm_b_re[l],
              'ssm_b_im': ssm_b_im[l], 'ssm_c_re': ssm_c_re[l], 'ssm_c_im': ssm_c_im[l], 'ssm_d': ssm_d[l],
              'ssm_log_dt': ssm_log_dt[l], 'ssm_w_glu': ssm_w_glu[l], 'ssm_b_glu': ssm_b_glu[l],
              'conv_w': conv_w[l], 'conv_b': conv_b[l], 'pool_w': pool_w[l], 'pool_scale': pool_scale[l],
              'nsa_w_cmp_k': nsa_w_cmp_k[l], 'nsa_w_cmp_v': nsa_w_cmp_v[l], 'w_out': w_out[l],
              'ln1_g': ln1_g[l], 'ln1_b': ln1_b[l], 'w_up': w_up[l], 'w_down': w_down[l],
              'ln2_g': ln2_g[l], 'ln2_b': ln2_b[l]}
        xp, st = trunk_layer(
            xp, lp, 0,
            jnp.zeros((B, SSM_GROUPS, SSM_STATE), jnp.float32),
            jnp.zeros((B, SSM_GROUPS, SSM_STATE), jnp.float32),
            jnp.zeros((B, CONV_K - 1, GROUP_W), xp.dtype),
            jnp.zeros((B, POOL_BUF, GROUP_W), xp.dtype), 0,
            jnp.zeros((B, 0, 4, KV_HEADS, HEAD_DIM), xp.dtype),
            jnp.zeros((B, WINDOW, 2, KV_HEADS, HEAD_DIM), xp.dtype))
        kv_p.append(st[0]); win_p.append(st[1]); sre_p.append(st[2]); sim_p.append(st[3])
        conv_p.append(st[4]); pool_p.append(st[5])
        past = cache_nsa_kv[l][page_table].reshape(Bd, past_len, 4, KV_HEADS, HEAD_DIM)
        xs, st = trunk_layer(
            xs, lp, past_len, state_ssm_re[l], state_ssm_im[l], state_conv[l], state_pool[l],
            min(POOL_BUF, past_len), past, cache_win_kv[l])
        kv_s.append(st[0]); win_s.append(st[1]); sre_s.append(st[2]); sim_s.append(st[3])
        conv_s.append(st[4]); pool_s.append(st[5])
    return (xp, xs, jnp.stack(kv_p), jnp.stack(kv_s), jnp.stack(win_p), jnp.stack(win_s),
            jnp.stack(sre_p), jnp.stack(sim_p), jnp.stack(sre_s), jnp.stack(sim_s),
            jnp.stack(conv_p), jnp.stack(conv_s), jnp.stack(pool_p), jnp.stack(pool_s))
```

```python
import functools
import math

import jax
import jax.numpy as jnp
from jax import lax
from jax.experimental import pallas as pl
from jax.experimental.pallas import tpu as pltpu

F32 = jnp.float32
BF16 = jnp.bfloat16
HI = lax.Precision.HIGHEST

D_MODEL = 2048
GROUP_W = 512
SSM_GROUPS = 32
SSM_STATE = 64
SSM_CH = 16
CONV_K = 3
POOL_WINDOWS = (2, 4, 8, 16)
POOL_CH = 128
POOL_BUF = 15
HEAD_DIM = 64
KV_HEADS = 2
Q_PER_KV = 4
BLOCK = 64
TOP_N = 16
WINDOW = 512
D_FF = 8192
IN_COLS = 3864
IN_COLS_PAD = 4096
NEG = -1e30
FORCED = Q_PER_KV + 1.0
LN_EPS = 1e-5
SCALE = HEAD_DIM ** -0.5
MIB = 1024 * 1024

COL_U_SSM, COL_GB, COL_GC, COL_V, COL_U_POOL, COL_Q = 0, 512, 1024, 1536, 2048, 2560
COL_KV = 3072
COL_GATE = 3840


def _cp(sem, vmem_mb=48):
    return pltpu.CompilerParams(dimension_semantics=sem, vmem_limit_bytes=vmem_mb * MIB)


def _ln(z, g, b):
    mu = jnp.mean(z, axis=-1, keepdims=True)
    d = z - mu
    var = jnp.mean(d * d, axis=-1, keepdims=True)
    return d * lax.rsqrt(var + LN_EPS) * g + b


def _in_proj_kernel(x_ref, w_ref, o_ref):
    o_ref[...] = jnp.dot(x_ref[...], w_ref[...], preferred_element_type=F32)


def in_proj(x_bf, w_bf):
    M, D = x_bf.shape
    N = w_bf.shape[1]
    tm = min(M, 1024)
    tn = 512
    return pl.pallas_call(
        _in_proj_kernel,
        grid=(M // tm, N // tn),
        in_specs=[pl.BlockSpec((tm, D), lambda i, j: (i, 0)),
                  pl.BlockSpec((D, tn), lambda i, j: (0, j))],
        out_specs=pl.BlockSpec((tm, tn), lambda i, j: (i, j)),
        out_shape=jax.ShapeDtypeStruct((M, N), F32),
        compiler_params=_cp(("parallel", "arbitrary")),
        name="in_proj",
    )(x_bf, w_bf)


def _out_proj_kernel(ya_ref, yb_ref, yc_ref, yd_ref, w_ref, x_ref, g_ref, b_ref, o_ref, *, alpha):
    mix = jnp.dot(ya_ref[...], w_ref[0:512, :], preferred_element_type=F32)
    mix += jnp.dot(yb_ref[...], w_ref[512:1024, :], preferred_element_type=F32)
    mix += jnp.dot(yc_ref[...], w_ref[1024:1536, :], preferred_element_type=F32)
    mix += jnp.dot(yd_ref[...], w_ref[1536:2048, :], preferred_element_type=F32)
    o_ref[...] = _ln(alpha * x_ref[...] + mix, g_ref[...], b_ref[...])


def out_proj(ya, yb, yc, yd, w_bf, x, g, b, alpha):
    M = x.shape[0]
    tm = min(M, 512)
    ys = pl.BlockSpec((tm, GROUP_W), lambda i: (i, 0))
    vec = pl.BlockSpec((1, D_MODEL), lambda i: (0, 0))
    return pl.pallas_call(
        functools.partial(_out_proj_kernel, alpha=alpha),
        grid=(M // tm,),
        in_specs=[ys, ys, ys, ys,
                  pl.BlockSpec((D_MODEL, D_MODEL), lambda i: (0, 0)),
                  pl.BlockSpec((tm, D_MODEL), lambda i: (i, 0)), vec, vec],
        out_specs=pl.BlockSpec((tm, D_MODEL), lambda i: (i, 0)),
        out_shape=jax.ShapeDtypeStruct((M, D_MODEL), F32),
        compiler_params=_cp(("parallel",)),
        name="out_proj_ln",
    )(ya, yb, yc, yd, w_bf, x, g.reshape(1, -1), b.reshape(1, -1))


def _ffn_kernel(x_ref, wu_ref, wd_ref, g_ref, b_ref, o_ref, ob_ref, xb_ref, acc_ref, *, alpha):
    f = pl.program_id(1)

    @pl.when(f == 0)
    def _():
        xb_ref[...] = x_ref[...].astype(BF16)
        acc_ref[...] = jnp.zeros_like(acc_ref)

    h = jnp.dot(xb_ref[...], wu_ref[...], preferred_element_type=F32)
    h = jnp.square(jnp.maximum(h, 0.0))
    acc_ref[...] += jnp.dot(h.astype(BF16), wd_ref[...], preferred_element_type=F32)

    @pl.when(f == pl.num_programs(1) - 1)
    def _():
        y = _ln(alpha * x_ref[...] + acc_ref[...], g_ref[...], b_ref[...])
        o_ref[...] = y
        ob_ref[...] = y.astype(BF16)


def ffn(x, wu_bf, wd_bf, g, b, alpha):
    M = x.shape[0]
    tm = min(M, 512)
    tf = 512
    vec = pl.BlockSpec((1, D_MODEL), lambda i, f: (0, 0))
    return pl.pallas_call(
        functools.partial(_ffn_kernel, alpha=alpha),
        grid=(M // tm, D_FF // tf),
        in_specs=[pl.BlockSpec((tm, D_MODEL), lambda i, f: (i, 0)),
                  pl.BlockSpec((D_MODEL, tf), lambda i, f: (0, f)),
                  pl.BlockSpec((tf, D_MODEL), lambda i, f: (f, 0)), vec, vec],
        out_specs=[pl.BlockSpec((tm, D_MODEL), lambda i, f: (i, 0)),
                   pl.BlockSpec((tm, D_MODEL), lambda i, f: (i, 0))],
        out_shape=[jax.ShapeDtypeStruct((M, D_MODEL), F32),
                   jax.ShapeDtypeStruct((M, D_MODEL), BF16)],
        scratch_shapes=[pltpu.VMEM((tm, D_MODEL), BF16), pltpu.VMEM((tm, D_MODEL), F32)],
        compiler_params=_cp(("parallel", "arbitrary")),
        name="ffn_ln",
    )(x, wu_bf, wd_bf, g.reshape(1, -1), b.reshape(1, -1))


def _ssm_bu_kernel(u_ref, w_ref, o_ref):
    u = u_ref[0].astype(BF16)
    for k in range(4):
        r = jnp.dot(u[:, 128 * k:128 * (k + 1)], w_ref[k], preferred_element_type=F32)
        o_ref[0, :, 512 * k:512 * (k + 1)] = r[:, :512]
        o_ref[0, :, 2048 + 512 * k:2048 + 512 * (k + 1)] = r[:, 512:]


def ssm_bu(h3, w_bu_bf):
    B, T, _ = h3.shape
    tm = min(T, 512)
    return pl.pallas_call(
        _ssm_bu_kernel,
        grid=(B, T // tm),
        in_specs=[pl.BlockSpec((1, tm, GROUP_W), lambda b, t: (b, t, COL_U_SSM // GROUP_W)),
                  pl.BlockSpec((4, 128, 1024), lambda b, t: (0, 0, 0))],
        out_specs=pl.BlockSpec((1, tm, 4096), lambda b, t: (b, t, 0)),
        out_shape=jax.ShapeDtypeStruct((B, T, 4096), F32),
        compiler_params=_cp(("parallel", "parallel")),
        name="ssm_bu",
    )(h3, w_bu_bf)


def _ssm_scan_kernel(bu_ref, ab_ref, h_ref, last_ref, st_ref, *, tc):
    i = pl.program_id(0)

    @pl.when(i == 0)
    def _():
        st_ref[...] = jnp.zeros_like(st_ref)

    ar = ab_ref[0][None]
    ai = ab_ref[1][None]

    def step(t, carry):
        hr, hi = carry
        br = bu_ref[:, t, 0:16, :]
        bi = bu_ref[:, t, 16:32, :]
        nr = ar * hr - ai * hi + br
        ni = ar * hi + ai * hr + bi
        h_ref[:, t, 0:16, :] = nr
        h_ref[:, t, 16:32, :] = ni
        return nr, ni

    hr, hi = lax.fori_loop(0, tc, step, (st_ref[:, 0:16, :], st_ref[:, 16:32, :]), unroll=8)
    st_ref[:, 0:16, :] = hr
    st_ref[:, 16:32, :] = hi
    last_ref[:, 0:16, :] = hr
    last_ref[:, 16:32, :] = hi


def ssm_scan(bu, ab):
    B, T, _ = bu.shape
    tc = min(T, 128)
    bu4 = bu.reshape(B, T, 32, 128)
    h, last = pl.pallas_call(
        functools.partial(_ssm_scan_kernel, tc=tc),
        grid=(T // tc,),
        in_specs=[pl.BlockSpec((B, tc, 32, 128), lambda i: (0, i, 0, 0)),
                  pl.BlockSpec((2, 16, 128), lambda i: (0, 0, 0))],
        out_specs=[pl.BlockSpec((B, tc, 32, 128), lambda i: (0, i, 0, 0)),
                   pl.BlockSpec((B, 32, 128), lambda i: (0, 0, 0))],
        out_shape=[jax.ShapeDtypeStruct((B, T, 32, 128), F32),
                   jax.ShapeDtypeStruct((B, 32, 128), F32)],
        scratch_shapes=[pltpu.VMEM((B, 32, 128), F32)],
        compiler_params=_cp(("arbitrary",)),
        name="ssm_scan",
    )(bu4, ab)
    return h.reshape(B, T, 4096), last


def _gelu_glu(y, wg, bg):
    y = jax.nn.gelu(y)
    z = jnp.dot(y.astype(BF16), wg, preferred_element_type=F32) + bg
    return y * jax.nn.sigmoid(z)


def _ssm_y_kernel(h_ref, u_ref, c_ref, d_ref, wg_ref, bg_ref, o_ref):
    parts = []
    for k in range(4):
        hre = h_ref[0, :, 512 * k:512 * (k + 1)].astype(BF16)
        him = h_ref[0, :, 2048 + 512 * k:2048 + 512 * (k + 1)].astype(BF16)
        yk = jnp.dot(hre, c_ref[k, 0:512, :], preferred_element_type=F32)
        yk += jnp.dot(him, c_ref[k, 512:1024, :], preferred_element_type=F32)
        parts.append(yk)
    y = jnp.concatenate(parts, axis=1) + d_ref[...] * u_ref[0]
    o_ref[0] = _gelu_glu(y, wg_ref[...], bg_ref[...]).astype(BF16)


def ssm_y(hst, h3, c_bf, d, wg_bf, bg):
    B, T, _ = hst.shape
    tm = min(T, 512)
    vec = pl.BlockSpec((1, GROUP_W), lambda b, t: (0, 0))
    return pl.pallas_call(
        _ssm_y_kernel,
        grid=(B, T // tm),
        in_specs=[pl.BlockSpec((1, tm, 4096), lambda b, t: (b, t, 0)),
                  pl.BlockSpec((1, tm, GROUP_W), lambda b, t: (b, t, COL_U_SSM // GROUP_W)),
                  pl.BlockSpec((4, 1024, 128), lambda b, t: (0, 0, 0)), vec,
                  pl.BlockSpec((GROUP_W, GROUP_W), lambda b, t: (0, 0)), vec],
        out_specs=pl.BlockSpec((1, tm, GROUP_W), lambda b, t: (b, t, 0)),
        out_shape=jax.ShapeDtypeStruct((B, T, GROUP_W), BF16),
        compiler_params=_cp(("parallel", "parallel")),
        name="ssm_y",
    )(hst, h3, c_bf, d.reshape(1, -1), wg_bf, bg.reshape(1, -1))


HALO = 16


def _convpool_kernel(gb_ref, gc_ref, v_ref, up_ref, cw_ref, cb_ref, pw_ref, ps_ref,
                     yb_ref, yc_ref, nconv_ref, cv_ext, up_ext, *, tm):
    t = pl.program_id(1)

    @pl.when(t == 0)
    def _():
        cv_ext[0:HALO, :] = jnp.zeros((HALO, GROUP_W), F32)
        up_ext[0:HALO, :] = jnp.zeros((HALO, GROUP_W), F32)

    @pl.when(t > 0)
    def _():
        cv_ext[0:HALO, :] = cv_ext[tm:tm + HALO, :]
        up_ext[0:HALO, :] = up_ext[tm:tm + HALO, :]

    cv = gc_ref[0] * v_ref[0]
    cv_ext[HALO:HALO + tm, :] = cv
    up = up_ref[0]
    up_ext[HALO:HALO + tm, :] = up

    conv = cb_ref[...] + cw_ref[2:3, :] * cv
    conv += cw_ref[1:2, :] * cv_ext[HALO - 1:HALO - 1 + tm, :]
    conv += cw_ref[0:1, :] * cv_ext[HALO - 2:HALO - 2 + tm, :]
    yb_ref[0] = (gb_ref[0] * conv).astype(BF16)
    nconv_ref[0] = cv[tm - (CONV_K - 1):, :]

    tpos = t * tm + lax.broadcasted_iota(jnp.int32, (tm, POOL_CH), 0)
    outs = []
    for gi, w in enumerate(POOL_WINDOWS):
        sl = slice(gi * POOL_CH, (gi + 1) * POOL_CH)
        s = up[:, sl]
        for k in range(1, w):
            s = s + up_ext[HALO - k:HALO - k + tm, sl]
        cnt = jnp.minimum(tpos + 1, w).astype(F32)
        pooled = s / cnt - up[:, sl]
        outs.append(jnp.dot(pooled.astype(BF16), pw_ref[gi], preferred_element_type=F32))
    yc_ref[0] = (jnp.concatenate(outs, axis=1) * ps_ref[...]).astype(BF16)


def convpool(h3, conv_w, conv_b, pool_w_bf, pool_scale):
    B, T, _ = h3.shape
    tm = min(T, 512)

    def col(c):
        return pl.BlockSpec((1, tm, GROUP_W), lambda b, t: (b, t, c // GROUP_W))

    vec = pl.BlockSpec((1, GROUP_W), lambda b, t: (0, 0))
    return pl.pallas_call(
        functools.partial(_convpool_kernel, tm=tm),
        grid=(B, T // tm),
        in_specs=[col(COL_GB), col(COL_GC), col(COL_V), col(COL_U_POOL),
                  pl.BlockSpec((CONV_K, GROUP_W), lambda b, t: (0, 0)), vec,
                  pl.BlockSpec((4, POOL_CH, POOL_CH), lambda b, t: (0, 0, 0)), vec],
        out_specs=[pl.BlockSpec((1, tm, GROUP_W), lambda b, t: (b, t, 0)),
                   pl.BlockSpec((1, tm, GROUP_W), lambda b, t: (b, t, 0)),
                   pl.BlockSpec((1, CONV_K - 1, GROUP_W), lambda b, t: (b, 0, 0))],
        out_shape=[jax.ShapeDtypeStruct((B, T, GROUP_W), BF16),
                   jax.ShapeDtypeStruct((B, T, GROUP_W), BF16),
                   jax.ShapeDtypeStruct((B, CONV_K - 1, GROUP_W), F32)],
        scratch_shapes=[pltpu.VMEM((HALO + tm, GROUP_W), F32), pltpu.VMEM((HALO + tm, GROUP_W), F32)],
        compiler_params=_cp(("parallel", "arbitrary")),
        name="conv_pool",
    )(h3, h3, h3, h3, conv_w, conv_b.reshape(1, -1), pool_w_bf, pool_scale.reshape(1, -1))


def _compress_kernel(kv_ref, w_ref, o_ref):
    x = kv_ref[0, :, 0:256]
    nb = x.shape[0] // BLOCK
    x = x.reshape(nb, BLOCK, 256) * w_ref[...][None]
    o_ref[0] = jnp.sum(x, axis=1)


def compress_prompt(h3, w_cmp):
    B, T, _ = h3.shape
    tm = min(T, 512)
    return pl.pallas_call(
        _compress_kernel,
        grid=(B, T // tm),
        in_specs=[pl.BlockSpec((1, tm, 512), lambda b, t: (b, t, COL_KV // 512)),
                  pl.BlockSpec((BLOCK, 256), lambda b, t: (0, 0))],
        out_specs=pl.BlockSpec((1, tm // BLOCK, 256), lambda b, t: (b, t, 0)),
        out_shape=jax.ShapeDtypeStruct((B, T // BLOCK, 256), F32),
        compiler_params=_cp(("parallel", "parallel")),
        name="nsa_compress",
    )(h3, w_cmp)


def _rank_select(score_t, nb_iter):
    jrow = lax.broadcasted_iota(jnp.int32, score_t.shape, 0)
    cnt = jnp.zeros(score_t.shape, jnp.int32)
    for jp in range(nb_iter):
        row = score_t[jp:jp + 1, :]
        ge = (row >= score_t).astype(jnp.int32)
        gt = (row > score_t).astype(jnp.int32)
        cnt = cnt + jnp.where(jrow > jp, ge, gt)
    return cnt < TOP_N


def _cmp_kernel(q_ref, kc_ref, ocmp_ref, qa_ref, *, tq, nb):
    g = pl.program_id(1)
    i = pl.program_id(2)
    R = Q_PER_KV * tq
    q = q_ref[0]
    qs = jnp.concatenate([q[:, r * 64:(r + 1) * 64] for r in range(Q_PER_KV)], axis=0)
    kcvc = kc_ref[0]
    kc = jnp.where(g == 0, kcvc[:, 0:64], kcvc[:, 64:128])
    vc = jnp.where(g == 0, kcvc[:, 128:192], kcvc[:, 192:256])
    nbp = kcvc.shape[0]
    s = lax.dot_general(qs, kc, (((1,), (1,)), ((), ())), precision=HI,
                        preferred_element_type=F32) * SCALE
    qpos = i * tq + lax.broadcasted_iota(jnp.int32, (R, nbp), 0) % tq
    j = lax.broadcasted_iota(jnp.int32, (R, nbp), 1)
    complete = ((j + 1) * BLOCK - 1 <= qpos) & (j < nb)
    s = jnp.where(complete, s, NEG)
    m = jnp.max(s, axis=-1, keepdims=True)
    e = jnp.where(complete, jnp.exp(s - m), 0.0)
    den = jnp.sum(e, axis=-1, keepdims=True)
    p = e / jnp.where(den > 0.0, den, 1.0)
    o = jnp.dot(p, vc, precision=HI, preferred_element_type=F32)
    ocmp_ref[0] = jnp.concatenate([o[r * tq:(r + 1) * tq] for r in range(Q_PER_KV)], axis=1)
    imp = p[0:tq] + p[tq:2 * tq] + p[2 * tq:3 * tq] + p[3 * tq:4 * tq]
    tpos = i * tq + lax.broadcasted_iota(jnp.int32, (tq, nbp), 0)
    jj = lax.broadcasted_iota(jnp.int32, (tq, nbp), 1)
    cur = tpos // BLOCK
    forced = (jj == 0) | (jj == cur) | (jj == cur - 1)
    started = jj * BLOCK <= tpos
    score = jnp.where(forced, FORCED, jnp.where(started, imp, -1.0))
    sel_t = _rank_select(score.T, nb)
    bias = jnp.where(sel_t, 0.0, NEG).T[:, 0:64]
    bias4 = jnp.concatenate([bias] * Q_PER_KV, axis=0)
    qsc = qs * SCALE
    qa0 = jnp.concatenate([qsc, bias4], axis=1)
    qa1 = jnp.concatenate([bias4, qsc], axis=1)
    qa_ref[0, 0, 0] = jnp.where(g == 0, qa0, qa1).astype(BF16)


def cmp_attn(h3, kcvc_pad, nb):
    B, T, _ = h3.shape
    tq = 128
    nq = T // tq
    nbp = kcvc_pad.shape[1]
    return pl.pallas_call(
        functools.partial(_cmp_kernel, tq=tq, nb=nb),
        grid=(B, KV_HEADS, nq),
        in_specs=[pl.BlockSpec((1, tq, 256), lambda b, g, i: (b, i, COL_Q // 256 + g)),
                  pl.BlockSpec((1, nbp, 256), lambda b, g, i: (b, 0, 0))],
        out_specs=[pl.BlockSpec((1, tq, 256), lambda b, g, i: (b, i, g)),
                   pl.BlockSpec((1, 1, 1, Q_PER_KV * tq, 128), lambda b, g, i: (b, g, i, 0, 0))],
        out_shape=[jax.ShapeDtypeStruct((B, T, GROUP_W), F32),
                   jax.ShapeDtypeStruct((B, KV_HEADS, nq, Q_PER_KV * tq, 128), BF16)],
        compiler_params=_cp(("parallel", "parallel", "parallel")),
        name="nsa_cmp_topk",
    )(h3, kcvc_pad)


def _attn_kernel(qa_ref, ks_ref, vs_ref, kw_ref, vw_ref, oc_ref, gt_ref, yd_ref, *, tq, tk, span):
    g = pl.program_id(1)
    i = pl.program_id(2)
    R = Q_PER_KV * tq
    q0 = i * tq
    qa = qa_ref[0, 0, 0]
    nt = (((1,), (1,)), ((), ()))

    def own_lanes(n):
        lane = lax.broadcasted_iota(jnp.int32, (n, 128), 1)
        return (lane < 64) == (g == 0)

    def sel_chunk(c, carry, masked):
        m, l, acc = carry
        k0 = pl.multiple_of(c * tk, tk)
        kb = ks_ref[0, pl.ds(k0, tk), :]
        lane = lax.broadcasted_iota(jnp.int32, (tk, 128), 1)
        jblk = c * (tk // BLOCK) + lax.broadcasted_iota(jnp.int32, (tk, 128), 0) // BLOCK
        e = jnp.where((lane % 64) == jblk, 1.0, 0.0)
        ka = jnp.where(own_lanes(tk), kb, e).astype(BF16)
        s = lax.dot_general(qa, ka, nt, preferred_element_type=F32)
        if masked:
            kpos = k0 + lax.broadcasted_iota(jnp.int32, (R, tk), 1)
            qpos = q0 + lax.broadcasted_iota(jnp.int32, (R, tk), 0) % tq
            s = jnp.where(kpos <= qpos, s, NEG)
        mn = jnp.maximum(m, jnp.max(s, axis=-1, keepdims=True))
        a = jnp.exp(m - mn)
        p = jnp.exp(s - mn)
        l = a * l + jnp.sum(p, axis=-1, keepdims=True)
        vb = vs_ref[0, pl.ds(k0, tk), :].astype(BF16)
        acc = a * acc + jnp.dot(p.astype(BF16), vb, preferred_element_type=F32)
        return mn, l, acc

    c_last = q0 // tk
    init = (jnp.full((R, 1), NEG, F32), jnp.zeros((R, 1), F32), jnp.zeros((R, 128), F32))
    carry = lax.fori_loop(0, c_last, lambda c, cr: sel_chunk(c, cr, False), init)
    m, l, acc = sel_chunk(c_last, carry, True)
    osel = acc / l

    w0 = pl.multiple_of(jnp.maximum(q0 + tq - span, 0), tq)
    kb = kw_ref[0, pl.ds(w0, span), :]
    ka = jnp.where(own_lanes(span), kb, 0.0).astype(BF16)
    s = lax.dot_general(qa, ka, nt, preferred_element_type=F32)
    kpos = w0 + lax.broadcasted_iota(jnp.int32, (R, span), 1)
    qpos = q0 + lax.broadcasted_iota(jnp.int32, (R, span), 0) % tq
    rel = qpos - kpos
    s = jnp.where((rel >= 0) & (rel < WINDOW), s, NEG)
    m = jnp.max(s, axis=-1, keepdims=True)
    p = jnp.exp(s - m)
    l = jnp.sum(p, axis=-1, keepdims=True)
    vb = vw_ref[0, pl.ds(w0, span), :].astype(BF16)
    owin = jnp.dot(p.astype(BF16), vb, preferred_element_type=F32) / l

    osel = jnp.where(g == 0, osel[:, 0:64], osel[:, 64:128])
    owin = jnp.where(g == 0, owin[:, 0:64], owin[:, 64:128])
    gl = jax.nn.sigmoid(gt_ref[0])
    gsel = jnp.where(g == 0, gl[:, 0:12], gl[:, 12:24])
    oc = oc_ref[0]
    outs = []
    for r in range(Q_PER_KV):
        y = gsel[:, 3 * r:3 * r + 1] * oc[:, r * 64:(r + 1) * 64]
        y += gsel[:, 3 * r + 1:3 * r + 2] * osel[r * tq:(r + 1) * tq]
        y += gsel[:, 3 * r + 2:3 * r + 3] * owin[r * tq:(r + 1) * tq]
        outs.append(y)
    yd_ref[0] = jnp.concatenate(outs, axis=1).astype(BF16)


def nsa_attn(h3, qa, ocmp):
    B, T, _ = h3.shape
    tq = 128
    nq = T // tq
    tk = min(T, 512)
    span = WINDOW + tq
    assert T >= span

    def kvcol(slot):
        return pl.BlockSpec((1, T, 128), lambda b, g, i: (b, 0, COL_KV // 128 + slot))

    return pl.pallas_call(
        functools.partial(_attn_kernel, tq=tq, tk=tk, span=span),
        grid=(B, KV_HEADS, nq),
        in_specs=[pl.BlockSpec((1, 1, 1, Q_PER_KV * tq, 128), lambda b, g, i: (b, g, i, 0, 0)),
                  kvcol(2), kvcol(3), kvcol(4), kvcol(5),
                  pl.BlockSpec((1, tq, 256), lambda b, g, i: (b, i, g)),
                  pl.BlockSpec((1, tq, 128), lambda b, g, i: (b, i, COL_GATE // 128))],
        out_specs=pl.BlockSpec((1, tq, 256), lambda b, g, i: (b, i, g)),
        out_shape=jax.ShapeDtypeStruct((B, T, GROUP_W), BF16),
        compiler_params=_cp(("parallel", "parallel", "parallel")),
        name="nsa_sel_win",
    )(qa, h3, h3, h3, h3, ocmp, h3)


def _ssm_params(lp):
    a_re = lp['ssm_a_re'].astype(F32)
    a_im = lp['ssm_a_im'].astype(F32)
    dt = jnp.exp(lp['ssm_log_dt'].astype(F32))[:, None]
    mag = jnp.exp(dt * a_re)
    ab_re = mag * jnp.cos(dt * a_im)
    ab_im = mag * jnp.sin(dt * a_im)
    den = a_re * a_re + a_im * a_im
    nr = ab_re - 1.0
    f_re = (nr * a_re + ab_im * a_im) / den
    f_im = (ab_im * a_re - nr * a_im) / den
    b_re = lp['ssm_b_re'].astype(F32)
    b_im = lp['ssm_b_im'].astype(F32)
    bb_re = f_re[..., None] * b_re - f_im[..., None] * b_im
    bb_im = f_re[..., None] * b_im + f_im[..., None] * b_re
    eye = jnp.eye(8, dtype=F32)

    def bd_in(bb):
        return jnp.einsum('kgnc,gh->kgchn', bb.reshape(4, 8, SSM_STATE, SSM_CH), eye).reshape(4, 128, 512)

    def bd_out(c):
        return jnp.einsum('kgcn,gh->khngc', c.reshape(4, 8, SSM_CH, SSM_STATE), eye).reshape(4, 512, 128)

    w_bu = jnp.concatenate([bd_in(bb_re), bd_in(bb_im)], axis=2)
    w_c = jnp.concatenate([bd_out(lp['ssm_c_re'].astype(F32)), -bd_out(lp['ssm_c_im'].astype(F32))], axis=1)
    ab = jnp.stack([ab_re.reshape(16, 128), ab_im.reshape(16, 128)])
    return w_bu, w_c, ab


def _smix_kernel(h_ref, sre_ref, sim_ref, cst_ref, pst_ref, wbu_ref, ab_ref, wc_ref, d_ref, wg_ref, bg_ref,
                 cw_ref, cb_ref, pw_ref, ps_ref,
                 ya_ref, yb_ref, yc_ref, nre_ref, nim_ref, ncv_ref, npl_ref, *, nb, nt, n_prev_valid):
    u = h_ref[:, COL_U_SSM:COL_U_SSM + GROUP_W]
    bu = [jnp.dot(u[:, 128 * k:128 * (k + 1)], wbu_ref[k], precision=HI, preferred_element_type=F32)
          for k in range(4)]
    bu_re = jnp.concatenate([x[:, :512] for x in bu], axis=1)
    bu_im = jnp.concatenate([x[:, 512:] for x in bu], axis=1)
    ar = ab_ref[0:1, :]
    ai = ab_ref[1:2, :]
    hr = sre_ref[...]
    hi = sim_ref[...]
    hrs, his = [], []
    for t in range(nt):
        sl = slice(t * nb, (t + 1) * nb)
        nr = ar * hr - ai * hi + bu_re[sl]
        ni = ar * hi + ai * hr + bu_im[sl]
        hr, hi = nr, ni
        hrs.append(hr)
        his.append(hi)
    nre_ref[...] = hr
    nim_ref[...] = hi
    hre = jnp.concatenate(hrs, axis=0).astype(BF16)
    him = jnp.concatenate(his, axis=0).astype(BF16)
    parts = []
    for k in range(4):
        yk = jnp.dot(hre[:, 512 * k:512 * (k + 1)], wc_ref[k, 0:512, :], preferred_element_type=F32)
        yk += jnp.dot(him[:, 512 * k:512 * (k + 1)], wc_ref[k, 512:1024, :], preferred_element_type=F32)
        parts.append(yk)
    y = jnp.concatenate(parts, axis=1) + d_ref[...] * u
    ya_ref[...] = _gelu_glu(y, wg_ref[...], bg_ref[...]).astype(BF16)

    gb = h_ref[:, COL_GB:COL_GB + GROUP_W]
    cv = h_ref[:, COL_GC:COL_GC + GROUP_W] * h_ref[:, COL_V:COL_V + GROUP_W]
    ext = [cst_ref[j] for j in range(CONV_K - 1)] + [cv[t * nb:(t + 1) * nb] for t in range(nt)]
    convs = []
    for t in range(nt):
        c = cb_ref[...]
        for j in range(CONV_K):
            c = c + cw_ref[j:j + 1, :] * ext[t + j]
        convs.append(c)
    yb_ref[...] = (gb * jnp.concatenate(convs, axis=0)).astype(BF16)
    for j in range(CONV_K - 1):
        ncv_ref[j] = ext[len(ext) - (CONV_K - 1) + j]

    up = h_ref[:, COL_U_POOL:COL_U_POOL + GROUP_W]
    pext = [pst_ref[j] for j in range(POOL_BUF)] + [up[t * nb:(t + 1) * nb] for t in range(nt)]
    rows = []
    for t in range(nt):
        outs = []
        for gi, w in enumerate(POOL_WINDOWS):
            sl = slice(gi * POOL_CH, (gi + 1) * POOL_CH)
            s = pext[POOL_BUF + t][:, sl]
            for k in range(1, w):
                s = s + pext[POOL_BUF + t - k][:, sl]
            cnt = float(min(t + 1 + n_prev_valid, w))
            outs.append(s / cnt - pext[POOL_BUF + t][:, sl])
        rows.append(jnp.concatenate(outs, axis=1))
    pooled = jnp.concatenate(rows, axis=0)
    ycs = [jnp.dot(pooled[:, gi * POOL_CH:(gi + 1) * POOL_CH].astype(BF16), pw_ref[gi],
                   preferred_element_type=F32) for gi in range(4)]
    yc_ref[...] = (jnp.concatenate(ycs, axis=1) * ps_ref[...]).astype(BF16)
    for j in range(POOL_BUF):
        npl_ref[j] = pext[len(pext) - POOL_BUF + j]


def sample_mix(h_s, sre, sim, cst, pst, w_bu, ab, w_c_bf, d, wg_bf, bg, conv_w, conv_b, pool_w_bf, pool_scale,
               nb, nt, n_prev_valid):
    M = nb * nt
    outs = pl.pallas_call(
        functools.partial(_smix_kernel, nb=nb, nt=nt, n_prev_valid=n_prev_valid),
        out_shape=[jax.ShapeDtypeStruct((M, GROUP_W), BF16)] * 3 + [
            jax.ShapeDtypeStruct((nb, 2048), F32), jax.ShapeDtypeStruct((nb, 2048), F32),
            jax.ShapeDtypeStruct((CONV_K - 1, nb, GROUP_W), F32),
            jax.ShapeDtypeStruct((POOL_BUF, nb, GROUP_W), F32)],
        compiler_params=pltpu.CompilerParams(vmem_limit_bytes=48 * MIB),
        name="sample_mixers",
    )(h_s, sre, sim, cst, pst, w_bu, ab.reshape(2, 2048), w_c_bf, d.reshape(1, -1), wg_bf, bg.reshape(1, -1),
      conv_w, conv_b.reshape(1, -1), pool_w_bf, pool_scale.reshape(1, -1))
    return outs


PAGES_PER_STEP = 8


def _scompress_kernel(pt_ref, *refs):
    w_ref = refs[PAGES_PER_STEP]
    o_ref = refs[PAGES_PER_STEP + 1]
    outs = []
    for i in range(PAGES_PER_STEP):
        x = refs[i][...]
        nblk = x.shape[0] // BLOCK
        x = x.reshape(nblk, BLOCK, 256) * w_ref[...][None]
        outs.append(jnp.sum(x, axis=1))
    o_ref[0] = jnp.concatenate(outs, axis=0)


def _page_spec(layer, i, colblk):
    def imap(b, p, pt):
        return (layer, pt[b, p * PAGES_PER_STEP + i], 0, colblk)
    return imap


def sample_compress(cache4, page_table, w_cmp, layer):
    nbat, n_pages = page_table.shape
    page = cache4.shape[2]
    bpp = page // BLOCK
    steps = n_pages // PAGES_PER_STEP
    in_specs = [pl.BlockSpec((None, None, page, 256), _page_spec(layer, i, 0)) for i in range(PAGES_PER_STEP)]
    in_specs.append(pl.BlockSpec((BLOCK, 256), lambda b, p, pt: (0, 0)))
    return pl.pallas_call(
        _scompress_kernel,
        grid_spec=pltpu.PrefetchScalarGridSpec(
            num_scalar_prefetch=1, grid=(nbat, steps), in_specs=in_specs,
            out_specs=pl.BlockSpec((1, PAGES_PER_STEP * bpp, 256), lambda b, p, pt: (b, p, 0))),
        out_shape=jax.ShapeDtypeStruct((nbat, n_pages * bpp, 256), F32),
        compiler_params=_cp(("parallel", "arbitrary")),
        name="sample_compress",
    )(page_table, *([cache4] * PAGES_PER_STEP), w_cmp)


def _stop_kernel(q_ref, kc_ref, ssum_ref, ocmp_ref, sel_ref, *, nbat, nt, nb, past_len):
    nq = Q_PER_KV * nt
    bdims = (((2,), (2,)), ((0,), (0,)))
    nbs = sel_ref.shape[1]
    for g in range(KV_HEADS):
        q = q_ref[g]
        kc = kc_ref[:, :, g * 64:(g + 1) * 64]
        vc = kc_ref[:, :, 128 + g * 64:128 + (g + 1) * 64]
        s = lax.dot_general(q, kc, bdims, precision=HI, preferred_element_type=F32) * SCALE
        row = lax.broadcasted_iota(jnp.int32, (nbat, nq, nb), 1)
        j = lax.broadcasted_iota(jnp.int32, (nbat, nq, nb), 2)
        qpos = past_len + row % nt
        complete = (j + 1) * BLOCK - 1 <= qpos
        s = jnp.where(complete, s, NEG)
        m = jnp.max(s, axis=-1, keepdims=True)
        e = jnp.where(complete, jnp.exp(s - m), 0.0)
        den = jnp.sum(e, axis=-1, keepdims=True)
        p = e / jnp.where(den > 0.0, den, 1.0)
        ocmp_ref[g] = lax.dot_general(p, vc, (((2,), (1,)), ((0,), (0,))), precision=HI,
                                      preferred_element_type=F32)
        imp = jnp.dot(ssum_ref[...], p.reshape(nbat * nq, nb), precision=HI,
                      preferred_element_type=F32)
        n = nbat * nt
        imp = jnp.concatenate([imp, jnp.zeros((n, 2 * 128 - nb), F32)], axis=1)
        tpos = past_len + lax.broadcasted_iota(jnp.int32, (n, 256), 0) % nt
        jj = lax.broadcasted_iota(jnp.int32, (n, 256), 1)
        cur = tpos // BLOCK
        forced = (jj == 0) | (jj == cur) | (jj == cur - 1)
        started = jj * BLOCK <= tpos
        score = jnp.where(forced, FORCED, jnp.where(started, imp, -1.0))
        score_t = jnp.concatenate([score[:, 0:128].T, score[:, 128:256].T], axis=0)[0:nbs]
        sel = _rank_select(score_t, nb + 1)
        sel_ref[g] = jnp.where(sel, 0.0, NEG)


def sample_topk(q4, kcvc, ssum, nbat, nt, nb, past_len, nbs):
    return pl.pallas_call(
        functools.partial(_stop_kernel, nbat=nbat, nt=nt, nb=nb, past_len=past_len),
        out_shape=[jax.ShapeDtypeStruct((KV_HEADS, nbat, Q_PER_KV * nt, HEAD_DIM), F32),
                   jax.ShapeDtypeStruct((KV_HEADS, nbs, nbat * nt), F32)],
        compiler_params=pltpu.CompilerParams(vmem_limit_bytes=48 * MIB),
        name="sample_cmp_topk",
    )(q4, kcvc, ssum)


def _ssel_kernel(pt_ref, *refs, nt, past_len, n_prev):
    P = PAGES_PER_STEP
    pages = refs[:P]
    (qbd_ref, bias_ref, e_ref, nbias_ref, newkv_ref, win_ref, ocmp_ref, gate_ref,
     y_ref, m_ref, l_ref, acc_ref) = refs[P:]
    pg = pl.program_id(1)
    nt_ = (((1,), (1,)), ((), ()))
    R = qbd_ref.shape[1]

    @pl.when(pg == 0)
    def _():
        m_ref[...] = jnp.full_like(m_ref, NEG)
        l_ref[...] = jnp.zeros_like(l_ref)
        acc_ref[...] = jnp.zeros_like(acc_ref)

    qbd = qbd_ref[0]
    slab = jnp.concatenate([p[...] for p in pages], axis=0)
    kb = slab[:, 0:128].astype(BF16)
    vb = slab[:, 128:256].astype(BF16)
    s = lax.dot_general(qbd, kb, nt_, preferred_element_type=F32)
    s = s + jnp.dot(bias_ref[0, 0].astype(BF16), e_ref[...], preferred_element_type=F32)
    m = m_ref[...]
    mn = jnp.maximum(m, jnp.max(s, axis=-1, keepdims=True))
    a = jnp.exp(m - mn)
    p = jnp.exp(s - mn)
    l_ref[...] = a * l_ref[...] + jnp.sum(p, axis=-1, keepdims=True)
    acc_ref[...] = a * acc_ref[...] + jnp.dot(p.astype(BF16), vb, preferred_element_type=F32)
    m_ref[...] = mn

    @pl.when(pg == pl.num_programs(1) - 1)
    def _():
        rowt = lax.broadcasted_iota(jnp.int32, (R, 8), 0) % nt
        col = lax.broadcasted_iota(jnp.int32, (R, 8), 1)
        new_ok = col <= rowt
        nk = newkv_ref[0]
        sn = lax.dot_general(qbd, nk[:, 0:128].astype(BF16), nt_, preferred_element_type=F32)
        sn = jnp.where(new_ok, sn + nbias_ref[0], NEG)
        m0 = m_ref[...]
        m1 = jnp.maximum(m0, jnp.max(sn, axis=-1, keepdims=True))
        a1 = jnp.exp(m0 - m1)
        pn = jnp.exp(sn - m1)
        l1 = a1 * l_ref[...] + jnp.sum(pn, axis=-1, keepdims=True)
        acc1 = a1 * acc_ref[...] + jnp.dot(pn.astype(BF16), nk[:, 128:256].astype(BF16),
                                           preferred_element_type=F32)
        osel = acc1 / l1
        win = win_ref[0]
        sw = lax.dot_general(qbd, win[:, 0:128].astype(BF16), nt_, preferred_element_type=F32)
        c = lax.broadcasted_iota(jnp.int32, (R, n_prev), 1)
        qp = past_len + lax.broadcasted_iota(jnp.int32, (R, n_prev), 0) % nt
        rel = qp - (past_len - n_prev + c)
        sw = jnp.where((rel >= 0) & (rel < WINDOW), sw, NEG)
        swn = lax.dot_general(qbd, nk[:, 256:384].astype(BF16), nt_, preferred_element_type=F32)
        swn = jnp.where(new_ok, swn, NEG)
        mw = jnp.maximum(jnp.max(sw, axis=-1, keepdims=True), jnp.max(swn, axis=-1, keepdims=True))
        pw = jnp.exp(sw - mw)
        pwn = jnp.exp(swn - mw)
        lw = jnp.sum(pw, axis=-1, keepdims=True) + jnp.sum(pwn, axis=-1, keepdims=True)
        ow = jnp.dot(pw.astype(BF16), win[:, 128:256].astype(BF16), preferred_element_type=F32)
        ow += jnp.dot(pwn.astype(BF16), nk[:, 384:512].astype(BF16), preferred_element_type=F32)
        owin = ow / lw
        rg = lax.broadcasted_iota(jnp.int32, (R, 64), 0) < (R // 2)
        osel = jnp.where(rg, osel[:, 0:64], osel[:, 64:128])
        owin = jnp.where(rg, owin[:, 0:64], owin[:, 64:128])
        gl = jax.nn.sigmoid(gate_ref[0])
        y_ref[0] = gl[:, 0:1] * ocmp_ref[0] + gl[:, 1:2] * osel + gl[:, 2:3] * owin


def sample_sel(cache4, page_table, qbd, bias, e_mat, nbias, newkv, win, ocmp, gates, layer, nt, past_len):
    nbat, n_pages = page_table.shape
    page = cache4.shape[2]
    steps = n_pages // PAGES_PER_STEP
    R = qbd.shape[1]
    n_prev = win.shape[1]
    keys = PAGES_PER_STEP * page
    in_specs = [pl.BlockSpec((None, None, page, 256), _page_spec(layer, i, 1)) for i in range(PAGES_PER_STEP)]
    in_specs += [
        pl.BlockSpec((1, R, 128), lambda b, p, pt: (b, 0, 0)),
        pl.BlockSpec((1, 1, R, 128), lambda b, p, pt: (b, p, 0, 0)),
        pl.BlockSpec((128, keys), lambda b, p, pt: (0, 0)),
        pl.BlockSpec((1, R, 8), lambda b, p, pt: (b, 0, 0)),
        pl.BlockSpec((1, 8, 512), lambda b, p, pt: (b, 0, 0)),
        pl.BlockSpec((1, n_prev, 256), lambda b, p, pt: (b, 0, 0)),
        pl.BlockSpec((1, R, 64), lambda b, p, pt: (b, 0, 0)),
        pl.BlockSpec((1, R, 128), lambda b, p, pt: (b, 0, 0)),
    ]
    return pl.pallas_call(
        functools.partial(_ssel_kernel, nt=nt, past_len=past_len, n_prev=n_prev),
        grid_spec=pltpu.PrefetchScalarGridSpec(
            num_scalar_prefetch=1, grid=(nbat, steps), in_specs=in_specs,
            out_specs=pl.BlockSpec((1, R, 64), lambda b, p, pt: (b, 0, 0)),
            scratch_shapes=[pltpu.VMEM((R, 1), F32), pltpu.VMEM((R, 1), F32), pltpu.VMEM((R, 128), F32)]),
        out_shape=jax.ShapeDtypeStruct((nbat, R, 64), F32),
        compiler_params=_cp(("parallel", "arbitrary")),
        name="sample_sel_win",
    )(page_table, *([cache4] * PAGES_PER_STEP), qbd, bias, e_mat, nbias, newkv, win, ocmp, gates)


def _prep_layer(lp):
    w_in = jnp.pad(lp['w_in'], ((0, 0), (0, IN_COLS_PAD - IN_COLS))).astype(BF16)
    w_bu, w_c, ab = _ssm_params(lp)
    w_cmp = jnp.concatenate([lp['nsa_w_cmp_k']] * 2 + [lp['nsa_w_cmp_v']] * 2, axis=1).astype(F32)
    return dict(
        w_in=w_in, w_bu=w_bu, w_bu_bf=w_bu.astype(BF16), w_c_bf=w_c.astype(BF16), ab=ab,
        d=lp['ssm_d'], wg_bf=lp['ssm_w_glu'].astype(BF16), bg=lp['ssm_b_glu'],
        conv_w=lp['conv_w'], conv_b=lp['conv_b'], pool_w_bf=lp['pool_w'].astype(BF16),
        pool_scale=lp['pool_scale'], w_cmp=w_cmp, w_out=lp['w_out'].astype(BF16),
        ln1_g=lp['ln1_g'], ln1_b=lp['ln1_b'], w_up=lp['w_up'].astype(BF16), w_down=lp['w_down'].astype(BF16),
        ln2_g=lp['ln2_g'], ln2_b=lp['ln2_b'])


def _prompt_layer(x, x_bf, pp, alpha):
    B, T, _ = x.shape
    M = B * T
    h = in_proj(x_bf, pp['w_in'])
    h3 = h.reshape(B, T, IN_COLS_PAD)
    bu = ssm_bu(h3, pp['w_bu_bf'])
    hst, last = ssm_scan(bu, pp['ab'])
    ya = ssm_y(hst, h3, pp['w_c_bf'], pp['d'], pp['wg_bf'], pp['bg'])
    yb, yc, nconv = convpool(h3, pp['conv_w'], pp['conv_b'], pp['pool_w_bf'], pp['pool_scale'])
    nb = T // BLOCK
    assert nb <= 64
    kcvc = compress_prompt(h3, pp['w_cmp'])
    kcvc = jnp.pad(kcvc, ((0, 0), (0, 128 - nb), (0, 0)))
    ocmp, qa = cmp_attn(h3, kcvc, nb)
    yd = nsa_attn(h3, qa, ocmp)
    x1 = out_proj(ya.reshape(M, -1), yb.reshape(M, -1), yc.reshape(M, -1), yd.reshape(M, -1),
                  pp['w_out'], x.reshape(M, -1), pp['ln1_g'], pp['ln1_b'], alpha)
    x2, x2_bf = ffn(x1, pp['w_up'], pp['w_down'], pp['ln2_g'], pp['ln2_b'], alpha)
    kv = h3[:, :, COL_KV:COL_KV + 768].reshape(B, T, 6, KV_HEADS, HEAD_DIM)
    n_keep = min(WINDOW, T)
    state = (kv[:, :, :4], kv[:, -n_keep:, 4:],
             last[:, 0:16].reshape(B, SSM_GROUPS, SSM_STATE), last[:, 16:32].reshape(B, SSM_GROUPS, SSM_STATE),
             nconv, h3[:, -POOL_BUF:, COL_U_POOL:COL_U_POOL + GROUP_W])
    return x2.reshape(B, T, -1), x2_bf, state


def _sample_layer(x_tm, x_bf, pp, layer, alpha, cache4, page_table, win_prev, sre, sim, cst, pst, nbat, nt):
    n_pages = page_table.shape[1]
    page = cache4.shape[2]
    past_len = n_pages * page
    nb = past_len // BLOCK
    n_prev = win_prev.shape[1]
    h = in_proj(x_bf, pp['w_in'])
    ya, yb, yc, nre, nim, ncv, npl = sample_mix(
        h, sre.reshape(nbat, -1), sim.reshape(nbat, -1), cst.transpose(1, 0, 2), pst.transpose(1, 0, 2),
        pp['w_bu'], pp['ab'], pp['w_c_bf'], pp['d'], pp['wg_bf'], pp['bg'], pp['conv_w'], pp['conv_b'],
        pp['pool_w_bf'], pp['pool_scale'], nbat, nt, min(POOL_BUF, past_len))
    kcvc = sample_compress(cache4, page_table, pp['w_cmp'], layer)
    q = h[:, COL_Q:COL_Q + GROUP_W].reshape(nt, nbat, KV_HEADS, Q_PER_KV, HEAD_DIM)
    q4 = q.transpose(2, 1, 3, 0, 4).reshape(KV_HEADS, nbat, Q_PER_KV * nt, HEAD_DIM)
    nq = Q_PER_KV * nt
    qi = jnp.arange(nbat * nq)
    ri = jnp.arange(nbat * nt)
    ssum = ((qi[None, :] // nq == ri[:, None] // nt) & (qi[None, :] % nt == ri[:, None] % nt)).astype(F32)
    nbs = ((nb + 1 + 7) // 8) * 8
    ocmp, selb = sample_topk(q4, kcvc, ssum, nbat, nt, nb, past_len, nbs)
    sb = selb.reshape(KV_HEADS, nbs, nbat, nt).transpose(2, 0, 3, 1)
    sb = jnp.broadcast_to(sb[:, :, None], (nbat, KV_HEADS, Q_PER_KV, nt, nbs)).reshape(nbat, 2 * nq, nbs)
    bpg = PAGES_PER_STEP * page // BLOCK
    steps = n_pages // PAGES_PER_STEP
    bias = sb[:, :, :nb].reshape(nbat, 2 * nq, steps, bpg).transpose(0, 2, 1, 3)
    bias = jnp.pad(bias, ((0, 0), (0, 0), (0, 0), (0, 128 - bpg)))
    nbias = jnp.broadcast_to(sb[:, :, nb:nb + 1], (nbat, 2 * nq, 8))
    keys = PAGES_PER_STEP * page
    e_mat = (jnp.arange(128)[:, None] == (jnp.arange(keys)[None, :] // BLOCK)).astype(BF16)
    qsc = q4 * SCALE
    z = jnp.zeros_like(qsc[0])
    qbd = jnp.concatenate([jnp.concatenate([qsc[0], z], axis=-1),
                           jnp.concatenate([z, qsc[1]], axis=-1)], axis=1).astype(BF16)
    kvn = h[:, COL_KV:COL_KV + 768].reshape(nt, nbat, 6, KV_HEADS * HEAD_DIM)
    newkv = kvn[:, :, 2:6].transpose(1, 0, 2, 3).reshape(nbat, nt, 512)
    newkv = jnp.pad(newkv, ((0, 0), (0, 8 - nt), (0, 0)))
    gl = h[:, COL_GATE:COL_GATE + 24].reshape(nt, nbat, KV_HEADS, Q_PER_KV, 3)
    gl = gl.transpose(1, 2, 3, 0, 4).reshape(nbat, 2 * nq, 3)
    gl = jnp.pad(gl, ((0, 0), (0, 0), (0, 125)))
    ocmp_b = ocmp.transpose(1, 0, 2, 3).reshape(nbat, 2 * nq, HEAD_DIM)
    win = win_prev.reshape(nbat, n_prev, 256)
    y = sample_sel(cache4, page_table, qbd, bias, e_mat, nbias, newkv, win, ocmp_b, gl, layer, nt, past_len)
    yd = y.reshape(nbat, KV_HEADS, Q_PER_KV, nt, HEAD_DIM).transpose(3, 0, 1, 2, 4).reshape(nt * nbat, GROUP_W)
    x1 = out_proj(ya, yb, yc, yd.astype(BF16), pp['w_out'], x_tm, pp['ln1_g'], pp['ln1_b'], alpha)
    x2, x2_bf = ffn(x1, pp['w_up'], pp['w_down'], pp['ln2_g'], pp['ln2_b'], alpha)
    kv_bt = kvn.transpose(1, 0, 2, 3).reshape(nbat, nt, 6, KV_HEADS, HEAD_DIM)
    win_ext = jnp.concatenate([win_prev, kv_bt[:, :, 4:]], axis=1)
    n_keep = min(n_prev, past_len + nt)
    up_bt = h[:, COL_U_POOL:COL_U_POOL + GROUP_W].reshape(nt, nbat, GROUP_W)
    state = (kv_bt[:, :, :4], win_ext[:, -n_keep:],
             nre.reshape(nbat, SSM_GROUPS, SSM_STATE), nim.reshape(nbat, SSM_GROUPS, SSM_STATE),
             ncv.transpose(1, 0, 2), npl.transpose(1, 0, 2))
    return x2, x2_bf, state


def kernel(x_prompt, x_sample, cache_nsa_kv, cache_win_kv, state_ssm_re, state_ssm_im, state_conv, state_pool,
           page_table, w_in, ssm_a_re, ssm_a_im, ssm_b_re, ssm_b_im, ssm_c_re, ssm_c_im, ssm_d, ssm_log_dt,
           ssm_w_glu, ssm_b_glu, conv_w, conv_b, pool_w, pool_scale, nsa_w_cmp_k, nsa_w_cmp_v, w_out, ln1_g,
           ln1_b, w_up, w_down, ln2_g, ln2_b):
    depth = w_in.shape[0]
    alpha = (2.0 * depth) ** 0.25
    B, T, D = x_prompt.shape
    nbat, nt, _ = x_sample.shape
    n_pool, page = cache_nsa_kv.shape[1], cache_nsa_kv.shape[2]
    cache4 = cache_nsa_kv.reshape(depth, n_pool, page, 4 * KV_HEADS * HEAD_DIM)
    xp = x_prompt
    xp_bf = x_prompt.reshape(B * T, D).astype(BF16)
    xs = x_sample.transpose(1, 0, 2).reshape(nt * nbat, D)
    xs_bf = xs.astype(BF16)
    st_p, st_s = [], []
    for l in range(depth):
        lp = {'w_in': w_in[l], 'ssm_a_re': ssm_a_re[l], 'ssm_a_im': ssm_a_im[l], 'ssm_b_re': ssm_b_re[l],
              'ssm_b_im': ssm_b_im[l], 'ssm_c_re': ssm_c_re[l], 'ssm_c_im': ssm_c_im[l], 'ssm_d': ssm_d[l],
              'ssm_log_dt': ssm_log_dt[l], 'ssm_w_glu': ssm_w_glu[l], 'ssm_b_glu': ssm_b_glu[l],
              'conv_w': conv_w[l], 'conv_b': conv_b[l], 'pool_w': pool_w[l], 'pool_scale': pool_scale[l],
              'nsa_w_cmp_k': nsa_w_cmp_k[l], 'nsa_w_cmp_v': nsa_w_cmp_v[l], 'w_out': w_out[l],
              'ln1_g': ln1_g[l], 'ln1_b': ln1_b[l], 'w_up': w_up[l], 'w_down': w_down[l],
              'ln2_g': ln2_g[l], 'ln2_b': ln2_b[l]}
        pp = _prep_layer(lp)
        xp, xp_bf, sp = _prompt_layer(xp, xp_bf, pp, alpha)
        xs, xs_bf, ss = _sample_layer(xs, xs_bf, pp, l, alpha, cache4, page_table, cache_win_kv[l],
                                      state_ssm_re[l], state_ssm_im[l], state_conv[l], state_pool[l], nbat, nt)
        st_p.append(sp)
        st_s.append(ss)
    ys = xs.reshape(nt, nbat, D).transpose(1, 0, 2)
    def stk(sts, i):
        return jnp.stack([s[i] for s in sts])

    return (xp, ys, stk(st_p, 0), stk(st_s, 0), stk(st_p, 1), stk(st_s, 1),
            stk(st_p, 2), stk(st_p, 3), stk(st_s, 2), stk(st_s, 3),
            stk(st_p, 4), stk(st_s, 4), stk(st_p, 5), stk(st_s, 5))
```
